```python
import jax, jax.numpy as jnp
from jax import lax
import numpy as np

D_MODEL = 4096
BATCH = 2
SEQ = 4096
DEPTH = 4
DEC_BATCH = 1
DEC_SEQ = 8192
PAST_LEN = 128

MIX_WIDTH = D_MODEL
MLSTM_WIDTH = MIX_WIDTH // 2
N_MLSTM_HEADS = 4
MLSTM_HEAD_DIM = MLSTM_WIDTH // N_MLSTM_HEADS
CHUNK = 128
N_GATE_SETS = 4
POOL_WIDTH = MIX_WIDTH - MLSTM_WIDTH
POOL_WINDOWS = (2, 4, 8, 16)
N_POOL_GROUPS = len(POOL_WINDOWS)
POOL_GROUP_WIDTH = POOL_WIDTH // N_POOL_GROUPS
N_IN = 4 * MLSTM_WIDTH + N_GATE_SETS * N_MLSTM_HEADS + POOL_WIDTH
N_MEM = 256
N_CROSS_HEADS = 4
CROSS_HEAD_DIM = D_MODEL // N_CROSS_HEADS
D_FF = -(-(8 * D_MODEL) // (3 * 256)) * 256
N_NORMS = 7
EPS = 1e-6

kernel_name = "hybrid_mlstm_pool_encoder"


def rms_norm(x, g):
    xf = x.astype(jnp.float32)
    y = xf * lax.rsqrt(jnp.mean(xf * xf, axis=-1, keepdims=True) + EPS)
    return (y * g.astype(jnp.float32)).astype(x.dtype)


def mlstm_scan(q, k, v, log_i, log_f):
    B, H, L, dk = q.shape
    dv = v.shape[-1]
    nc = L // CHUNK

    def chunks(t):
        return jnp.moveaxis(t.reshape(t.shape[:2] + (nc, CHUNK) + t.shape[3:]), 2, 0)

    tril = jnp.tril(jnp.ones((CHUNK, CHUNK), dtype=bool))

    def step(carry, inp):
        C, n, m = carry
        qc, kc, vc, li, lf = inp
        b = jnp.cumsum(lf, axis=-1)
        dlog = jnp.where(tril, b[..., :, None] - b[..., None, :] + li[..., None, :], -jnp.inf)
        inter = b + m[..., None]
        m_t = jnp.maximum(inter, jnp.max(dlog, axis=-1))
        dmat = jnp.exp(dlog - m_t[..., None])
        w_inter = jnp.exp(inter - m_t)
        s = jnp.einsum('bhtd,bhsd->bhts', qc, kc) * dmat
        num = (w_inter[..., None] * jnp.einsum('bhtd,bhde->bhte', qc, C)
               + jnp.einsum('bhts,bhse->bhte', s, vc))
        den = w_inter * jnp.einsum('bhtd,bhd->bht', qc, n) + jnp.sum(s, axis=-1)
        h = num / jnp.maximum(jnp.abs(den), jnp.exp(-m_t))[..., None]
        g = b[..., -1]
        wlog = g[..., None] - b + li
        m_new = jnp.maximum(g + m, jnp.max(wlog, axis=-1))
        decay = jnp.exp(g + m - m_new)
        wk = jnp.exp(wlog - m_new[..., None])[..., None] * kc
        C = decay[..., None, None] * C + jnp.einsum('bhsd,bhse->bhde', wk, vc)
        n = decay[..., None] * n + jnp.sum(wk, axis=2)
        return (C, n, m_new), h

    init = (jnp.zeros((B, H, dk, dv), jnp.float32),
            jnp.zeros((B, H, dk), jnp.float32),
            jnp.zeros((B, H), jnp.float32))
    _, h = lax.scan(step, init, (chunks(q), chunks(k), chunks(v), chunks(log_i), chunks(log_f)))
    return jnp.moveaxis(h, 0, 2).reshape(B, H, L, dv)


def multiscale_pool(u, pool_w, pool_scale):
    B, L, _ = u.shape
    uf = u.astype(jnp.float32).reshape(B, L, N_POOL_GROUPS, POOL_GROUP_WIDTH)
    cs = jnp.concatenate([jnp.zeros((B, 1, N_POOL_GROUPS, POOL_GROUP_WIDTH), jnp.float32),
                          jnp.cumsum(uf, axis=1)], axis=1)
    pos = jnp.arange(L)
    outs = []
    for gi, w in enumerate(POOL_WINDOWS):
        lo = jnp.clip(pos - w // 2, 0, L)
        hi = jnp.clip(pos + w // 2, 0, L)
        csg = cs[:, :, gi]
        win_sum = jnp.take(csg, hi, axis=1) - jnp.take(csg, lo, axis=1)
        cnt = (hi - lo).astype(jnp.float32)[None, :, None]
        outs.append(win_sum / cnt - uf[:, :, gi])
    pooled = jnp.stack(outs, axis=2)
    mixed = jnp.einsum('blgc,gcd->blgd', pooled, pool_w.astype(jnp.float32))
    return (mixed.reshape(B, L, POOL_WIDTH) * pool_scale.astype(jnp.float32)).astype(u.dtype)


def hybrid_mixer(h, w_in, b_gates, head_norm_g, pool_w, pool_scale, w_out):
    B, L, _ = h.shape
    z = h @ w_in
    o1 = MLSTM_WIDTH
    q, k, v, o, gates, u = jnp.split(
        z, [o1, 2 * o1, 3 * o1, 4 * o1, 4 * o1 + N_GATE_SETS * N_MLSTM_HEADS], axis=-1)

    def heads(t):
        return t.astype(jnp.float32).reshape(B, L, N_MLSTM_HEADS, MLSTM_HEAD_DIM).transpose(0, 2, 1, 3)

    qh = heads(q) * (MLSTM_HEAD_DIM ** -0.5)
    kh, vh = heads(k), heads(v)
    gp = (gates.astype(jnp.float32).reshape(B, L, N_GATE_SETS, N_MLSTM_HEADS)
          + b_gates.astype(jnp.float32)).transpose(2, 0, 3, 1)
    li_f, lf_f = gp[0], jax.nn.log_sigmoid(gp[1])
    li_b, lf_b = gp[2], jax.nn.log_sigmoid(gp[3])
    h_fwd = mlstm_scan(qh, kh, vh, li_f, lf_f)
    flip = lambda t: jnp.flip(t, axis=2)
    h_bwd = flip(mlstm_scan(flip(qh), flip(kh), flip(vh), flip(li_b), flip(lf_b)))
    hm = h_fwd + h_bwd
    mu = jnp.mean(hm, axis=-1, keepdims=True)
    var = jnp.mean(jnp.square(hm - mu), axis=-1, keepdims=True)
    hm = (hm - mu) * lax.rsqrt(var + EPS)
    hm = hm * head_norm_g.astype(jnp.float32).reshape(N_MLSTM_HEADS, 1, MLSTM_HEAD_DIM)
    hm = hm.transpose(0, 2, 1, 3).reshape(B, L, MLSTM_WIDTH) * jax.nn.sigmoid(o.astype(jnp.float32))
    pooled = multiscale_pool(u, pool_w, pool_scale)
    return jnp.concatenate([hm.astype(h.dtype), pooled], axis=-1) @ w_out


def memory_cross_attention(h, mem_n, w_cq, w_ckv, w_co):
    B, L, _ = h.shape
    M = mem_n.shape[1]
    q = (h @ w_cq).reshape(B, L, N_CROSS_HEADS, CROSS_HEAD_DIM)
    k, v = jnp.split(mem_n @ w_ckv, 2, axis=-1)
    k = k.reshape(B, M, N_CROSS_HEADS, CROSS_HEAD_DIM)
    v = v.reshape(B, M, N_CROSS_HEADS, CROSS_HEAD_DIM)
    s = jnp.einsum('blhd,bmhd->bhlm', q, k).astype(jnp.float32) * (CROSS_HEAD_DIM ** -0.5)
    p = jax.nn.softmax(s, axis=-1).astype(v.dtype)
    out = jnp.einsum('bhlm,bmhd->blhd', p, v).reshape(B, L, D_MODEL)
    return out @ w_co


def swiglu(h, w_gu, w_down):
    g, u = jnp.split(h @ w_gu, 2, axis=-1)
    return (jax.nn.silu(g) * u) @ w_down


def trunk(x, mem, w_in, b_gates, head_norm_g, pool_w, pool_scale, w_out,
          w_cq, w_ckv, w_co, w_gu, w_down, norm_g):
    for l in range(DEPTH):
        ng = norm_g[l]
        mix = hybrid_mixer(rms_norm(x, ng[0]), w_in[l], b_gates[l], head_norm_g[l],
                           pool_w[l], pool_scale[l], w_out[l])
        x = x + rms_norm(mix, ng[1])
        mem_n = rms_norm(mem, ng[3])
        ca = memory_cross_attention(rms_norm(x, ng[2]), mem_n, w_cq[l], w_ckv[l], w_co[l])
        x = x + rms_norm(ca, ng[4])
        ff = swiglu(rms_norm(x, ng[5]), w_gu[l], w_down[l])
        x = x + rms_norm(ff, ng[6])
    return x


def setup_inputs(seed: int = 0) -> dict:
    key = jax.random.key(seed)
    ks = jax.random.split(key, 20)
    f32 = jnp.float32

    def w(k, shape, fan_in):
        return jax.random.normal(k, shape, f32) * (fan_in ** -0.5)

    i_bias = 0.1 * jax.random.normal(ks[5], (DEPTH, 2, N_MLSTM_HEADS), f32)
    f_bias = (jnp.linspace(3.0, 6.0, N_MLSTM_HEADS, dtype=f32)
              + 0.1 * jax.random.normal(ks[6], (DEPTH, 2, N_MLSTM_HEADS), f32))
    b_gates = jnp.stack([i_bias[:, 0], f_bias[:, 0], i_bias[:, 1], f_bias[:, 1]], axis=1)
    return {
        "x_prompt": jax.random.normal(ks[0], (BATCH, SEQ, D_MODEL), f32),
        "x_sample": jax.random.normal(ks[1], (DEC_BATCH, DEC_SEQ, D_MODEL), f32),
        "mem_prompt": jax.random.normal(ks[2], (BATCH, N_MEM, D_MODEL), f32),
        "mem_sample": jax.random.normal(ks[3], (DEC_BATCH, N_MEM, D_MODEL), f32),
        "w_in": w(ks[4], (DEPTH, D_MODEL, N_IN), D_MODEL),
        "b_gates": b_gates,
        "head_norm_g": 1.0 + 0.02 * jax.random.normal(ks[7], (DEPTH, MLSTM_WIDTH), f32),
        "pool_w": w(ks[8], (DEPTH, N_POOL_GROUPS, POOL_GROUP_WIDTH, POOL_GROUP_WIDTH), POOL_GROUP_WIDTH),
        "pool_scale": 1.0 + 0.02 * jax.random.normal(ks[9], (DEPTH, POOL_WIDTH), f32),
        "w_out": w(ks[10], (DEPTH, MIX_WIDTH, D_MODEL), MIX_WIDTH),
        "w_cq": w(ks[11], (DEPTH, D_MODEL, D_MODEL), D_MODEL),
        "w_ckv": w(ks[12], (DEPTH, D_MODEL, 2 * D_MODEL), D_MODEL),
        "w_co": w(ks[13], (DEPTH, D_MODEL, D_MODEL), D_MODEL),
        "w_gu": w(ks[14], (DEPTH, D_MODEL, 2 * D_FF), D_MODEL),
        "w_down": w(ks[15], (DEPTH, D_FF, D_MODEL), D_FF),
        "norm_g": 1.0 + 0.02 * jax.random.normal(ks[16], (DEPTH, N_NORMS, D_MODEL), f32),
    }


def reference(x_prompt, x_sample, mem_prompt, mem_sample, w_in, b_gates, head_norm_g,
              pool_w, pool_scale, w_out, w_cq, w_ckv, w_co, w_gu, w_down, norm_g):
    y_prompt = trunk(x_prompt, mem_prompt, w_in, b_gates, head_norm_g, pool_w, pool_scale,
                     w_out, w_cq, w_ckv, w_co, w_gu, w_down, norm_g)
    y_sample = trunk(x_sample, mem_sample, w_in, b_gates, head_norm_g, pool_w, pool_scale,
                     w_out, w_cq, w_ckv, w_co, w_gu, w_down, norm_g)
    return (y_prompt, y_sample)
```

```python
import functools

import jax
import jax.numpy as jnp
from jax import lax
from jax.experimental import pallas as pl
from jax.experimental.pallas import tpu as pltpu

N_MLSTM_HEADS = 4
CHUNK = 128
N_GATE_SETS = 4
POOL_WINDOWS = (2, 4, 8, 16)
N_CROSS_HEADS = 4
EPS = 1e-6

LANES = 128
HALO = 16
VMEM_LIMIT_BYTES = 56 * 1024 * 1024
F32 = jnp.float32
BF16 = jnp.bfloat16


def _tile(n, pref, align=LANES):
    t = (min(pref, n) // align) * align
    while t >= align:
        if n % t == 0:
            return t
        t -= align
    return n


def _params(*sem):
    return pltpu.CompilerParams(dimension_semantics=sem, vmem_limit_bytes=VMEM_LIMIT_BYTES)


def _mm_kernel(x_ref, w_ref, o_ref):
    o_ref[...] = jnp.dot(x_ref[...], w_ref[...], preferred_element_type=F32).astype(o_ref.dtype)


def _mm_acc_kernel(x_ref, w_ref, o_ref, acc_ref, *, nk):
    k = pl.program_id(2)

    @pl.when(k == 0)
    def _():
        acc_ref[...] = jnp.zeros_like(acc_ref)

    acc_ref[...] += jnp.dot(x_ref[...], w_ref[...], preferred_element_type=F32)

    @pl.when(k == nk - 1)
    def _():
        o_ref[...] = acc_ref[...].astype(o_ref.dtype)


def _matmul(x, w, out_dtype, *, tm=1024, tn=1024, tk=None, name="matmul"):
    m, kdim = x.shape
    n = w.shape[1]
    tm = _tile(m, tm, 8)
    tn = _tile(n, tn)
    tk = kdim if tk is None else _tile(kdim, tk)
    nk = kdim // tk
    if nk == 1:
        return pl.pallas_call(
            _mm_kernel,
            grid=(m // tm, n // tn),
            in_specs=[pl.BlockSpec((tm, kdim), lambda i, j: (i, 0)),
                      pl.BlockSpec((kdim, tn), lambda i, j: (0, j))],
            out_specs=pl.BlockSpec((tm, tn), lambda i, j: (i, j)),
            out_shape=jax.ShapeDtypeStruct((m, n), out_dtype),
            compiler_params=_params("parallel", "arbitrary"),
            name=name,
        )(x, w)
    return pl.pallas_call(
        functools.partial(_mm_acc_kernel, nk=nk),
        grid=(m // tm, n // tn, nk),
        in_specs=[pl.BlockSpec((tm, tk), lambda i, j, k: (i, k)),
                  pl.BlockSpec((tk, tn), lambda i, j, k: (k, j))],
        out_specs=pl.BlockSpec((tm, tn), lambda i, j, k: (i, j)),
        out_shape=jax.ShapeDtypeStruct((m, n), out_dtype),
        scratch_shapes=[pltpu.VMEM((tm, tn), F32)],
        compiler_params=_params("parallel", "arbitrary", "arbitrary"),
        name=name,
    )(x, w)


def _swiglu_kernel(x_ref, wg_ref, wu_ref, o_ref):
    x = x_ref[...]
    g = jnp.dot(x, wg_ref[...], preferred_element_type=F32)
    u = jnp.dot(x, wu_ref[...], preferred_element_type=F32)
    o_ref[...] = (g * jax.nn.sigmoid(g) * u).astype(o_ref.dtype)


def _swiglu_up(x, w_gu, fp, *, tm=1024, tn=512):
    m, kdim = x.shape
    tm = _tile(m, tm, 8)
    tn = _tile(fp, tn)
    nj = fp // tn
    return pl.pallas_call(
        _swiglu_kernel,
        grid=(m // tm, nj),
        in_specs=[pl.BlockSpec((tm, kdim), lambda i, j: (i, 0)),
                  pl.BlockSpec((kdim, tn), lambda i, j: (0, j)),
                  pl.BlockSpec((kdim, tn), lambda i, j: (0, j + nj))],
        out_specs=pl.BlockSpec((tm, tn), lambda i, j: (i, j)),
        out_shape=jax.ShapeDtypeStruct((m, fp), BF16),
        compiler_params=_params("parallel", "arbitrary"),
        name="swiglu_up",
    )(x, w_gu, w_gu)


def _rms(x, g):
    return x * lax.rsqrt(jnp.mean(x * x, axis=-1, keepdims=True) + EPS) * g


def _norm_kernel(x_ref, g_ref, h_ref):
    h_ref[...] = _rms(x_ref[...], g_ref[...]).astype(h_ref.dtype)


def _rmsnorm(x, g, *, tr=256):
    m, d = x.shape
    tr = _tile(m, tr, 8)
    return pl.pallas_call(
        _norm_kernel,
        grid=(m // tr,),
        in_specs=[pl.BlockSpec((tr, d), lambda i: (i, 0)),
                  pl.BlockSpec((1, d), lambda i: (0, 0))],
        out_specs=pl.BlockSpec((tr, d), lambda i: (i, 0)),
        out_shape=jax.ShapeDtypeStruct((m, d), BF16),
        compiler_params=_params("parallel"),
        name="rmsnorm",
    )(x, g.reshape(1, d))


def _resid_norm_kernel(x_ref, y_ref, gpost_ref, gpre_ref, xo_ref, h_ref):
    xn = x_ref[...] + _rms(y_ref[...], gpost_ref[...])
    xo_ref[...] = xn
    h_ref[...] = _rms(xn, gpre_ref[...]).astype(h_ref.dtype)


def _resid_kernel(x_ref, y_ref, gpost_ref, xo_ref):
    xo_ref[...] = x_ref[...] + _rms(y_ref[...], gpost_ref[...])


def _resid_norm(x, y, g_post, g_pre, *, tr=256):
    m, d = x.shape
    tr = _tile(m, tr, 8)
    row = pl.BlockSpec((tr, d), lambda i: (i, 0))
    vec = pl.BlockSpec((1, d), lambda i: (0, 0))
    if g_pre is None:
        return pl.pallas_call(
            _resid_kernel, grid=(m // tr,), in_specs=[row, row, vec], out_specs=row,
            out_shape=jax.ShapeDtypeStruct((m, d), F32),
            compiler_params=_params("parallel"), name="resid",
        )(x, y, g_post.reshape(1, d)), None
    return pl.pallas_call(
        _resid_norm_kernel, grid=(m // tr,), in_specs=[row, row, vec, vec], out_specs=[row, row],
        out_shape=[jax.ShapeDtypeStruct((m, d), F32), jax.ShapeDtypeStruct((m, d), BF16)],
        compiler_params=_params("parallel"), name="resid_norm",
    )(x, y, g_post.reshape(1, d), g_pre.reshape(1, d))


def _lane_scan(x, op, reverse):
    lane = lax.broadcasted_iota(jnp.int32, x.shape, 1)
    d = 1
    while d < LANES:
        if reverse:
            shifted, valid = pltpu.roll(x, LANES - d, axis=1), lane < LANES - d
        else:
            shifted, valid = pltpu.roll(x, d, axis=1), lane >= d
        x = jnp.where(valid, op(x, shifted), x)
        d *= 2
    return x


def _gate_kernel(gi_ref, gf_ref, out_ref, g_scr, amax_scr, mf_scr, mb_scr, *, starts, ends):
    nrows = gi_ref.shape[0]
    nc = nrows // 8
    li = gi_ref[...]
    gf = gf_ref[...]
    lf = jnp.minimum(gf, 0.0) - jnp.log1p(jnp.exp(-jnp.abs(gf)))
    bwd = (lax.broadcasted_iota(jnp.int32, li.shape, 0) & 7) >= N_MLSTM_HEADS
    b = jnp.where(bwd, _lane_scan(lf, jnp.add, True), _lane_scan(lf, jnp.add, False))
    a = li - b
    cm = jnp.where(bwd, _lane_scan(a, jnp.maximum, True), _lane_scan(a, jnp.maximum, False))
    amax = jnp.broadcast_to(jnp.max(a, axis=1, keepdims=True), a.shape)
    g_scr[...] = jnp.broadcast_to(jnp.sum(lf, axis=1, keepdims=True), a.shape)
    amax_scr[...] = amax

    def is_any(c, marks):
        return functools.reduce(jnp.logical_or, [c == s for s in marks])

    def body(c, carry):
        mf, mb = carry
        rf = pl.ds(pl.multiple_of(c * 8, 8), 8)
        mf = jnp.where(is_any(c, starts), 0.0, mf)
        mf_scr[rf, :] = mf
        mf = g_scr[rf, :] + jnp.maximum(mf, amax_scr[rf, :])
        cb = nc - 1 - c
        rb = pl.ds(pl.multiple_of(cb * 8, 8), 8)
        mb = jnp.where(is_any(cb, ends), 0.0, mb)
        mb_scr[rb, :] = mb
        mb = g_scr[rb, :] + jnp.maximum(mb, amax_scr[rb, :])
        return mf, mb

    zero = jnp.zeros((8, LANES), F32)
    lax.fori_loop(0, nc, body, (zero, zero))
    m = jnp.where(bwd, mb_scr[...], mf_scr[...])
    mt = jnp.maximum(m, cm)
    mlast = jnp.maximum(m, amax)
    out_ref[0] = a
    out_ref[1] = mt
    out_ref[2] = jnp.exp(m - mt)
    out_ref[3] = jnp.exp(-b - mt)
    out_ref[4] = jnp.exp(a - mlast)
    out_ref[5] = jnp.exp(m - mlast)


def _gate_prep(gates, b_gates, starts, ends):
    t = gates.shape[0]
    nc = t // CHUNK
    h = N_MLSTM_HEADS
    g = gates[:, :N_GATE_SETS * h] + b_gates.reshape(1, N_GATE_SETS * h)
    gt = g.T.reshape(N_GATE_SETS, h, nc, CHUNK)

    def rows(s_f, s_b):
        return jnp.concatenate([gt[s_f], gt[s_b]], axis=0).transpose(1, 0, 2).reshape(nc * 8, CHUNK)

    gi, gf = rows(0, 2), rows(1, 3)
    out = pl.pallas_call(
        functools.partial(_gate_kernel, starts=starts, ends=ends),
        out_shape=jax.ShapeDtypeStruct((6, nc * 8, CHUNK), F32),
        scratch_shapes=[pltpu.VMEM((nc * 8, CHUNK), F32)] * 4,
        compiler_params=pltpu.CompilerParams(vmem_limit_bytes=VMEM_LIMIT_BYTES),
        name="mlstm_gates",
    )(gi, gf)
    arow = out[0].reshape(nc, 8, CHUNK)
    colf = out[1:].reshape(5, nc, 8, CHUNK).transpose(1, 3, 2, 0).reshape(t, 8 * 5)
    return arow, colf


def _mlstm_kernel(qf_ref, kf_ref, vf_ref, qb_ref, kb_ref, vb_ref, af_ref, ab_ref, cf_ref, cb_ref,
                  hf_ref, hb_ref, c_scr, *, starts, ends, dh):
    c = pl.program_id(0)
    nc = pl.num_programs(0)
    scale = dh ** -0.5

    @pl.when(functools.reduce(jnp.logical_or, [c == s for s in starts]))
    def _():
        c_scr[0:N_MLSTM_HEADS] = jnp.zeros((N_MLSTM_HEADS,) + c_scr.shape[1:], F32)

    @pl.when(functools.reduce(jnp.logical_or, [nc - 1 - c == e for e in ends]))
    def _():
        c_scr[N_MLSTM_HEADS:] = jnp.zeros((N_MLSTM_HEADS,) + c_scr.shape[1:], F32)

    row = lax.broadcasted_iota(jnp.int32, (CHUNK, CHUNK), 0)
    col = lax.broadcasted_iota(jnp.int32, (CHUNK, CHUNK), 1)
    ones = jnp.ones((CHUNK, LANES), BF16)
    for d, (q_ref, k_ref, v_ref, a_ref, f_ref, h_ref) in enumerate(
            ((qf_ref, kf_ref, vf_ref, af_ref, cf_ref, hf_ref),
             (qb_ref, kb_ref, vb_ref, ab_ref, cb_ref, hb_ref))):
        mask = (col <= row) if d == 0 else (col >= row)
        feats = f_ref[...]
        for hd in range(N_MLSTM_HEADS):
            r = d * N_MLSTM_HEADS + hd
            hs = slice(hd * dh, (hd + 1) * dh)
            q, k, v = q_ref[:, hs], k_ref[:, hs], v_ref[:, hs]
            a_row = a_ref[r:r + 1, :]
            mt = feats[:, 5 * r + 0:5 * r + 1]
            w_inter = feats[:, 5 * r + 1:5 * r + 2]
            eneg = feats[:, 5 * r + 2:5 * r + 3]
            wks = feats[:, 5 * r + 3:5 * r + 4]
            decay = feats[0:1, 5 * r + 4:5 * r + 5]
            dmat = jnp.where(mask, jnp.exp(a_row - mt), 0.0)
            s = lax.dot_general(q, k, (((1,), (1,)), ((), ())), preferred_element_type=F32)
            p = (s * (dmat * scale)).astype(BF16)
            vext = jnp.concatenate([v, ones], axis=1)
            cst = c_scr[r]
            ext = (w_inter * scale) * jnp.dot(q, cst.astype(BF16), preferred_element_type=F32)
            ext = ext + jnp.dot(p, vext, preferred_element_type=F32)
            den = ext[:, dh:dh + 1]
            h_ref[:, hs] = ext[:, :dh] * (1.0 / jnp.maximum(jnp.abs(den), eneg))
            kw = (k.astype(F32) * wks).astype(BF16)
            upd = lax.dot_general(kw, vext, (((0,), (0,)), ((), ())), preferred_element_type=F32)
            c_scr[r] = decay * cst + upd


def _mlstm(z, arow, colf, w, starts, ends):
    t = z.shape[0]
    nc = t // CHUNK
    dh = w // N_MLSTM_HEADS
    fwd = lambda j: (lambda c: (c, j))
    bwd = lambda j: (lambda c: (nc - 1 - c, j))
    qkv = lambda f: [pl.BlockSpec((CHUNK, w), f(j)) for j in range(3)]
    nf = colf.shape[1]
    return pl.pallas_call(
        functools.partial(_mlstm_kernel, starts=starts, ends=ends, dh=dh),
        grid=(nc,),
        in_specs=qkv(fwd) + qkv(bwd) + [
            pl.BlockSpec((None, 8, CHUNK), lambda c: (c, 0, 0)),
            pl.BlockSpec((None, 8, CHUNK), lambda c: (nc - 1 - c, 0, 0)),
            pl.BlockSpec((CHUNK, nf), fwd(0)),
            pl.BlockSpec((CHUNK, nf), bwd(0))],
        out_specs=[pl.BlockSpec((CHUNK, w), fwd(0)), pl.BlockSpec((CHUNK, w), bwd(0))],
        out_shape=[jax.ShapeDtypeStruct((t, w), F32)] * 2,
        scratch_shapes=[pltpu.VMEM((2 * N_MLSTM_HEADS, dh, dh + LANES), F32)],
        compiler_params=_params("arbitrary"),
        name="mlstm_scan",
    )(z, z, z, z, z, z, arow, arow, colf, colf)


def _mixpost_kernel(hf_ref, hb_ref, o_ref, u_ref, up_ref, un_ref, hg_ref, pw_ref, ps_ref,
                    out_ref, ext_scr, *, tm, w, n_prompt_rows, seq, dec_seq):
    i = pl.program_id(0)
    dh = w // N_MLSTM_HEADS
    gw = w // len(POOL_WINDOWS)
    for hd in range(N_MLSTM_HEADS):
        hs = slice(hd * dh, (hd + 1) * dh)
        x = hf_ref[:, hs] + hb_ref[:, hs]
        mu = jnp.mean(x, axis=-1, keepdims=True)
        xc = x - mu
        var = jnp.mean(xc * xc, axis=-1, keepdims=True)
        y = xc * lax.rsqrt(var + EPS) * hg_ref[:, hs]
        out_ref[:, hs] = (y * jax.nn.sigmoid(o_ref[:, hs].astype(F32))).astype(out_ref.dtype)
    r0 = i * tm
    in_prompt = r0 < n_prompt_rows
    pos0 = jnp.where(in_prompt, r0 % seq, (r0 - n_prompt_rows) % dec_seq)
    slen = jnp.where(in_prompt, seq, dec_seq)
    has_prev = pos0 > 0
    has_next = pos0 + tm < slen
    u = u_ref[...].astype(F32)
    ext_scr[0:HALO, :] = jnp.where(has_prev, up_ref[...].astype(F32), 0.0)
    ext_scr[HALO:HALO + tm, :] = u
    ext_scr[HALO + tm:, :] = jnp.where(has_next, un_ref[...].astype(F32), 0.0)
    pos = pos0 + lax.broadcasted_iota(jnp.int32, (tm, 1), 0)
    for gi, win in enumerate(POOL_WINDOWS):
        gs = slice(gi * gw, (gi + 1) * gw)
        half = win // 2
        acc = ext_scr[HALO - half:HALO - half + tm, gs]
        for dd in range(-half + 1, half):
            acc = acc + ext_scr[HALO + dd:HALO + dd + tm, gs]
        cnt = (jnp.minimum(pos + half, slen) - jnp.maximum(pos - half, 0)).astype(F32)
        pooled = acc / cnt - u[:, gs]
        mixed = jnp.dot(pooled.astype(BF16), pw_ref[gi], preferred_element_type=F32)
        out_ref[:, w + gi * gw:w + (gi + 1) * gw] = (mixed * ps_ref[:, gs]).astype(out_ref.dtype)


def _mixpost(hf, hb, z, head_g, pool_w, pool_scale, w, n_prompt_rows, seq, dec_seq, *, tm=256):
    t = z.shape[0]
    tm = _tile(min(seq, dec_seq), tm, HALO)
    nh = tm // HALO
    last = t // HALO - 1
    row = lambda j: pl.BlockSpec((tm, w), lambda i: (i, j))
    vec = pl.BlockSpec((1, w), lambda i: (0, 0))
    g = len(POOL_WINDOWS)
    return pl.pallas_call(
        functools.partial(_mixpost_kernel, tm=tm, w=w, n_prompt_rows=n_prompt_rows, seq=seq,
                          dec_seq=dec_seq),
        grid=(t // tm,),
        in_specs=[row(0), row(0), row(3), row(4),
                  pl.BlockSpec((HALO, w), lambda i: (jnp.maximum(i * nh - 1, 0), 4)),
                  pl.BlockSpec((HALO, w), lambda i: (jnp.minimum((i + 1) * nh, last), 4)),
                  vec,
                  pl.BlockSpec((g, w // g, w // g), lambda i: (0, 0, 0)),
                  vec],
        out_specs=pl.BlockSpec((tm, 2 * w), lambda i: (i, 0)),
        out_shape=jax.ShapeDtypeStruct((t, 2 * w), BF16),
        scratch_shapes=[pltpu.VMEM((tm + 2 * HALO, w), F32)],
        compiler_params=_params("parallel"),
        name="headnorm_pool",
    )(hf, hb, z, z, z, z, head_g.reshape(1, w), pool_w, pool_scale.reshape(1, w))


def _attn_kernel(q_ref, k_ref, v_ref, o_ref, *, dh):
    scale = dh ** -0.5
    for hd in range(N_CROSS_HEADS):
        hs = slice(hd * dh, (hd + 1) * dh)
        s = lax.dot_general(q_ref[:, hs], k_ref[:, hs], (((1,), (1,)), ((), ())),
                            preferred_element_type=F32) * scale
        e = jnp.exp(s - jnp.max(s, axis=-1, keepdims=True))
        p = e * (1.0 / jnp.sum(e, axis=-1, keepdims=True))
        o_ref[:, hs] = jnp.dot(p.astype(BF16), v_ref[:, hs],
                               preferred_element_type=F32).astype(o_ref.dtype)


def _cross_attention(q, kv, n_mem, n_prompt_rows, seq, dec_seq, *, tm=512):
    t, d = q.shape
    tm = _tile(min(seq, dec_seq), tm, 8)
    n_prompt = n_prompt_rows // seq

    def batch(i):
        r0 = i * tm
        return jnp.where(r0 < n_prompt_rows, r0 // seq, n_prompt + (r0 - n_prompt_rows) // dec_seq)

    return pl.pallas_call(
        functools.partial(_attn_kernel, dh=d // N_CROSS_HEADS),
        grid=(t // tm,),
        in_specs=[pl.BlockSpec((tm, d), lambda i: (i, 0)),
                  pl.BlockSpec((n_mem, d), lambda i: (batch(i), 0)),
                  pl.BlockSpec((n_mem, d), lambda i: (batch(i), 1))],
        out_specs=pl.BlockSpec((tm, d), lambda i: (i, 0)),
        out_shape=jax.ShapeDtypeStruct((t, d), BF16),
        compiler_params=_params("parallel"),
        name="cross_attention",
    )(q, kv, kv)


def kernel(x_prompt, x_sample, mem_prompt, mem_sample, w_in, b_gates, head_norm_g, pool_w,
           pool_scale, w_out, w_cq, w_ckv, w_co, w_gu, w_down, norm_g):
    nb, seq, d = x_prompt.shape
    dec_nb, dec_seq, _ = x_sample.shape
    n_mem = mem_prompt.shape[1]
    depth = w_in.shape[0]
    w = head_norm_g.shape[1]
    ngate = N_GATE_SETS * N_MLSTM_HEADS
    d_ff = w_down.shape[1]
    fp = -(-d_ff // 1024) * 1024
    n_prompt_rows = nb * seq
    assert seq % CHUNK == 0 and dec_seq % CHUNK == 0 and w_in.shape[2] == 5 * w + ngate

    bounds = [b * seq for b in range(nb)] + [n_prompt_rows + b * dec_seq for b in range(dec_nb)]
    bounds.append(n_prompt_rows + dec_nb * dec_seq)
    starts = tuple(b // CHUNK for b in bounds[:-1])
    ends = tuple(b // CHUNK - 1 for b in bounds[1:])

    x = jnp.concatenate([x_prompt.reshape(-1, d), x_sample.reshape(-1, d)], axis=0)
    mem = jnp.concatenate([mem_prompt.reshape(-1, d), mem_sample.reshape(-1, d)], axis=0)

    w_main = jnp.concatenate([w_in[:, :, :4 * w], w_in[:, :, 4 * w + ngate:]], axis=2).astype(BF16)
    w_gate = jnp.pad(w_in[:, :, 4 * w:4 * w + ngate], ((0, 0), (0, 0), (0, LANES - ngate))).astype(BF16)
    pad_f = ((0, 0), (0, 0), (0, fp - d_ff))
    w_gu_p = jnp.concatenate([jnp.pad(w_gu[:, :, :d_ff], pad_f), jnp.pad(w_gu[:, :, d_ff:], pad_f)],
                             axis=2).astype(BF16)
    w_down_p = jnp.pad(w_down, ((0, 0), (0, fp - d_ff), (0, 0))).astype(BF16)
    w_out_b, w_cq_b, w_ckv_b, w_co_b = (a.astype(BF16) for a in (w_out, w_cq, w_ckv, w_co))
    pool_w_b = pool_w.astype(BF16)

    h = _rmsnorm(x, norm_g[0, 0])
    for l in range(depth):
        ng = norm_g[l]
        z = _matmul(h, w_main[l], BF16, name="in_proj")
        gates = _matmul(h, w_gate[l], F32, tn=LANES, name="gate_proj")
        arow, colf = _gate_prep(gates, b_gates[l], starts, ends)
        hf, hb = _mlstm(z, arow, colf, w, starts, ends)
        cat = _mixpost(hf, hb, z, head_norm_g[l], pool_w_b[l], pool_scale[l], w,
                       n_prompt_rows, seq, dec_seq)
        mix = _matmul(cat, w_out_b[l], F32, name="out_proj")
        x, hq = _resid_norm(x, mix, ng[1], ng[2])
        mem_n = _rmsnorm(mem, ng[3])
        kv = _matmul(mem_n, w_ckv_b[l], BF16, name="kv_proj")
        q = _matmul(hq, w_cq_b[l], BF16, name="q_proj")
        att = _cross_attention(q, kv, n_mem, n_prompt_rows, seq, dec_seq)
        ca = _matmul(att, w_co_b[l], F32, name="co_proj")
        x, hff = _resid_norm(x, ca, ng[4], ng[5])
        act = _swiglu_up(hff, w_gu_p[l], fp)
        ff = _matmul(act, w_down_p[l], F32, tk=fp // 4, name="down_proj")
        x, h = _resid_norm(x, ff, ng[6], norm_g[l + 1, 0] if l + 1 < depth else None)

    y_prompt = x[:n_prompt_rows].reshape(nb, seq, d)
    y_sample = x[n_prompt_rows:].reshape(dec_nb, dec_seq, d)
    return (y_prompt, y_sample)
```
